```python
import jax, jax.numpy as jnp
from jax import lax
import numpy as np

D_MODEL = 2048
BATCH = 1
SEQ = 8192
DEPTH = 1
DEC_BATCH = 32
DEC_SEQ = 1
PAST_LEN = 16384
PAGE_SIZE = 128

HEAD_DIM = 128
D_MIX = D_MODEL
H_ATTN = 8
D_ATTN = H_ATTN * HEAD_DIM
D_CONV = D_MIX - D_ATTN
CONV_GROUPS = D_CONV // HEAD_DIM
CONV_WIDTH = 3
PLE_DIM = 256
Q_BLOCK = 128
SPLIT_SIZES = (D_ATTN, D_ATTN, D_ATTN, H_ATTN, D_ATTN, D_CONV, D_CONV, D_CONV, D_CONV)
D_IN = 4 * D_ATTN + H_ATTN + 4 * D_CONV

kernel_name = "fox_shortconv_parallel_heads_step"


def _split_points():
    pts, acc = [], 0
    for s in SPLIT_SIZES[:-1]:
        acc += s
        pts.append(acc)
    return tuple(pts)


def _rms_norm(x, gain, eps=1e-6):
    xf = x.astype(jnp.float32)
    xf = xf * lax.rsqrt(jnp.mean(xf * xf, axis=-1, keepdims=True) + eps)
    return xf.astype(x.dtype) * gain


def _branch_inputs(x, norm_g, w_in, b_f, q_g, k_g):
    bsz, s = x.shape[0], x.shape[1]
    xn = _rms_norm(x, norm_g)
    z = xn @ w_in
    q, k, v, f_logit, g_attn, xc, b_gate, c_gate, g_conv = jnp.split(z, _split_points(), axis=-1)
    q = _rms_norm(q.reshape(bsz, s, H_ATTN, HEAD_DIM), q_g)
    k = _rms_norm(k.reshape(bsz, s, H_ATTN, HEAD_DIM), k_g)
    v = v.reshape(bsz, s, H_ATTN, HEAD_DIM)
    logf = jax.nn.log_sigmoid(f_logit.astype(jnp.float32) + b_f.astype(jnp.float32))
    u = c_gate * xc
    return q, k, v, logf, g_attn, u, b_gate, g_conv


def _fox_prompt(q, k, v, logf):
    b, s, h, d = q.shape
    f_t = jnp.cumsum(logf, axis=1).transpose(0, 2, 1)
    kpos = jnp.arange(s)
    scale = HEAD_DIM ** -0.5

    def one_block(i):
        start = i * Q_BLOCK
        qb = lax.dynamic_slice_in_dim(q, start, Q_BLOCK, axis=1)
        fq = lax.dynamic_slice_in_dim(f_t, start, Q_BLOCK, axis=2)
        logits = jnp.einsum('bqhd,bkhd->bhqk', qb, k).astype(jnp.float32) * scale
        bias = fq[..., :, None] - f_t[..., None, :]
        qpos = start + jnp.arange(Q_BLOCK)
        mask = kpos[None, :] <= qpos[:, None]
        logits = jnp.where(mask, logits + bias, -jnp.inf)
        p = jax.nn.softmax(logits, axis=-1)
        return jnp.einsum('bhqk,bkhd->bqhd', p.astype(v.dtype), v)

    out = lax.map(one_block, jnp.arange(s // Q_BLOCK))
    return out.transpose(1, 0, 2, 3, 4).reshape(b, s, h * d)


def _fox_sample(q, k_new, v_new, logf_new, k_past, v_past, logf_past):
    b, ds = q.shape[0], q.shape[1]
    past = k_past.shape[1]
    scale = HEAD_DIM ** -0.5
    f_all = jnp.cumsum(jnp.concatenate([logf_past.astype(jnp.float32), logf_new], axis=1), axis=1)
    fp = f_all[:, :past].transpose(0, 2, 1)
    fn = f_all[:, past:].transpose(0, 2, 1)
    lp = jnp.einsum('bqhd,bkhd->bhqk', q, k_past).astype(jnp.float32) * scale
    lp = lp + (fn[..., :, None] - fp[..., None, :])
    ln = jnp.einsum('bqhd,bkhd->bhqk', q, k_new).astype(jnp.float32) * scale
    ln = ln + (fn[..., :, None] - fn[..., None, :])
    mask = jnp.arange(ds)[None, :] <= jnp.arange(ds)[:, None]
    ln = jnp.where(mask, ln, -jnp.inf)
    p = jax.nn.softmax(jnp.concatenate([lp, ln], axis=-1), axis=-1)
    pp, pn = p[..., :past], p[..., past:]
    out = (jnp.einsum('bhqk,bkhd->bqhd', pp.astype(v_past.dtype), v_past)
           + jnp.einsum('bhqk,bkhd->bqhd', pn.astype(v_new.dtype), v_new))
    return out.reshape(b, ds, H_ATTN * HEAD_DIM)


def _short_conv(u, prefix, w):
    s = u.shape[1]
    up = jnp.concatenate([prefix.astype(u.dtype), u], axis=1)
    y = w[0] * up[:, 0:s]
    for j in range(1, CONV_WIDTH):
        y = y + w[j] * up[:, j:j + s]
    return y, up[:, -(CONV_WIDTH - 1):]


def _merge(x, attn, g_attn, conv, b_gate, g_conv, w_out, p, w_ple, w_ple_gate):
    y_attn = attn * jax.nn.silu(g_attn)
    y_conv = (b_gate * conv) * jax.nn.silu(g_conv)
    h = x + jnp.concatenate([y_attn, y_conv], axis=-1) @ w_out
    return h + jax.nn.sigmoid(h @ w_ple_gate) * (p @ w_ple)


def setup_inputs(seed: int = 0) -> dict:
    key = jax.random.key(seed)
    ks = jax.random.split(key, 20)
    n_pages = PAST_LEN // PAGE_SIZE
    n_used = DEC_BATCH * n_pages
    n_pool = n_used + max(1, n_used // 4)
    f32 = jnp.float32
    nrm = lambda k, shp: jax.random.normal(k, shp, f32)
    page_table = jax.random.permutation(ks[7], n_pool)[:n_used].reshape(DEC_BATCH, n_pages).astype(jnp.int32)
    return {
        "x_prompt": nrm(ks[0], (BATCH, SEQ, D_MODEL)),
        "x_sample": nrm(ks[1], (DEC_BATCH, DEC_SEQ, D_MODEL)),
        "cache_k": nrm(ks[2], (DEPTH, n_pool, PAGE_SIZE, H_ATTN, HEAD_DIM)),
        "cache_v": nrm(ks[3], (DEPTH, n_pool, PAGE_SIZE, H_ATTN, HEAD_DIM)),
        "cache_logf": jax.nn.log_sigmoid(3.0 + 0.5 * nrm(ks[4], (DEPTH, n_pool, PAGE_SIZE, H_ATTN))),
        "state_conv": nrm(ks[5], (DEPTH, DEC_BATCH, CONV_WIDTH - 1, D_CONV)),
        "page_table": page_table,
        "p_prompt": nrm(ks[8], (DEPTH, BATCH, SEQ, PLE_DIM)),
        "p_sample": nrm(ks[9], (DEPTH, DEC_BATCH, DEC_SEQ, PLE_DIM)),
        "norm_gain": 1.0 + 0.02 * nrm(ks[10], (DEPTH, D_MODEL)),
        "w_in": nrm(ks[11], (DEPTH, D_MODEL, D_IN)) * D_MODEL ** -0.5,
        "b_forget": 3.0 + 0.5 * nrm(ks[12], (DEPTH, H_ATTN)),
        "q_norm_gain": 1.0 + 0.02 * nrm(ks[13], (DEPTH, HEAD_DIM)),
        "k_norm_gain": 1.0 + 0.02 * nrm(ks[14], (DEPTH, HEAD_DIM)),
        "conv_w": nrm(ks[15], (DEPTH, CONV_WIDTH, D_CONV)) * CONV_WIDTH ** -0.5,
        "w_out": nrm(ks[16], (DEPTH, D_MIX, D_MODEL)) * D_MIX ** -0.5,
        "w_ple": nrm(ks[17], (DEPTH, PLE_DIM, D_MODEL)) * PLE_DIM ** -0.5,
        "w_ple_gate": nrm(ks[18], (DEPTH, D_MODEL, D_MODEL)) * D_MODEL ** -0.5,
    }


def reference(x_prompt, x_sample, cache_k, cache_v, cache_logf, state_conv, page_table,
              p_prompt, p_sample, norm_gain, w_in, b_forget, q_norm_gain, k_norm_gain,
              conv_w, w_out, w_ple, w_ple_gate):
    db = x_sample.shape[0]
    past = page_table.shape[1] * PAGE_SIZE
    xp, xs = x_prompt, x_sample
    kp_l, vp_l, fp_l, cp_l, ks_l, vs_l, fs_l, cs_l = [], [], [], [], [], [], [], []
    for i in range(DEPTH):
        q, k, v, logf, g_a, u, b_g, g_c = _branch_inputs(xp, norm_gain[i], w_in[i], b_forget[i],
                                                          q_norm_gain[i], k_norm_gain[i])
        attn = _fox_prompt(q, k, v, logf)
        zero_prefix = jnp.zeros((xp.shape[0], CONV_WIDTH - 1, D_CONV), u.dtype)
        conv, conv_state = _short_conv(u, zero_prefix, conv_w[i])
        xp_next = _merge(xp, attn, g_a, conv, b_g, g_c, w_out[i], p_prompt[i], w_ple[i], w_ple_gate[i])
        kp_l.append(k); vp_l.append(v); fp_l.append(logf); cp_l.append(conv_state)

        q, k, v, logf, g_a, u, b_g, g_c = _branch_inputs(xs, norm_gain[i], w_in[i], b_forget[i],
                                                          q_norm_gain[i], k_norm_gain[i])
        k_past = cache_k[i, page_table].reshape(db, past, H_ATTN, HEAD_DIM)
        v_past = cache_v[i, page_table].reshape(db, past, H_ATTN, HEAD_DIM)
        logf_past = cache_logf[i, page_table].reshape(db, past, H_ATTN)
        attn = _fox_sample(q, k, v, logf, k_past, v_past, logf_past)
        conv, conv_state = _short_conv(u, state_conv[i], conv_w[i])
        xs_next = _merge(xs, attn, g_a, conv, b_g, g_c, w_out[i], p_sample[i], w_ple[i], w_ple_gate[i])
        ks_l.append(k); vs_l.append(v); fs_l.append(logf); cs_l.append(conv_state)
        xp, xs = xp_next, xs_next
    return (xp, xs, jnp.stack(kp_l), jnp.stack(vp_l), jnp.stack(fp_l), jnp.stack(cp_l),
            jnp.stack(ks_l), jnp.stack(vs_l), jnp.stack(fs_l), jnp.stack(cs_l))
```

```python
import functools

import jax
import jax.numpy as jnp
from jax import lax
from jax.experimental import pallas as pl
from jax.experimental.pallas import tpu as pltpu

F32 = jnp.float32
BF16 = jnp.bfloat16

D_MODEL = 2048
HEAD_DIM = 128
H_ATTN = 8
D_ATTN = H_ATTN * HEAD_DIM
D_CONV = D_MODEL - D_ATTN
PLE_DIM = 256
PAGE = 128
EPS = 1e-6
SCALE = HEAD_DIM ** -0.5
LANES = 128
N_GROUPS = 8
CHUNK = 256
VMEM_LIMIT = 56 * 1024 * 1024


def _log_sigmoid(z):
    return jnp.minimum(z, 0.0) - jnp.log1p(jnp.exp(-jnp.abs(z)))


def _silu(g):
    return g * jax.nn.sigmoid(g)


def _split3(x):
    a = x.astype(BF16)
    r = x - a.astype(F32)
    b = r.astype(BF16)
    c = (r - b.astype(F32)).astype(BF16)
    return a, b, c


def _dot(a, b):
    return jnp.dot(a, b, preferred_element_type=F32)


def _dot_nt(a, b):
    return lax.dot_general(a, b, (((1,), (1,)), ((), ())), preferred_element_type=F32)


def _head_rms(zh, gain):
    ms = jnp.mean(zh * zh, axis=-1, keepdims=True)
    return (zh * lax.rsqrt(ms + EPS)) * gain


def _normed_x(x_ref, gain_ref):
    x = x_ref[...]
    ms = jnp.mean(x * x, axis=-1, keepdims=True)
    return ((x * lax.rsqrt(ms + EPS)) * gain_ref[...]).astype(BF16)


def _chunks():
    return [(c * CHUNK, (c + 1) * CHUNK) for c in range(D_ATTN // CHUNK)]


def _heads_in_chunk():
    return [(h * HEAD_DIM, (h + 1) * HEAD_DIM) for h in range(CHUNK // HEAD_DIM)]


def _qkv_gate_epilogues(j, xn_s, w_ref, qg_ref, kg_ref, q_ref, k32_ref, k16_ref, v32_ref, v16_ref, sg_ref):
    @pl.when(j == 0)
    def _():
        for lo, hi in _chunks():
            z = _dot(xn_s[...], w_ref[:, lo:hi])
            for a, b in _heads_in_chunk():
                qn = _head_rms(z[:, a:b], qg_ref[...])
                q_ref[:, lo + a:lo + b] = (qn * SCALE).astype(q_ref.dtype)

    @pl.when(j == 1)
    def _():
        for lo, hi in _chunks():
            z = _dot(xn_s[...], w_ref[:, lo:hi])
            for a, b in _heads_in_chunk():
                kn = _head_rms(z[:, a:b], kg_ref[...])
                k32_ref[:, lo + a:lo + b] = kn
                if k16_ref is not None:
                    k16_ref[:, lo + a:lo + b] = kn.astype(BF16)

    @pl.when(j == 2)
    def _():
        for lo, hi in _chunks():
            z = _dot(xn_s[...], w_ref[:, lo:hi])
            v32_ref[:, lo:hi] = z
            if v16_ref is not None:
                v16_ref[:, lo:hi] = z.astype(BF16)

    @pl.when(j == 3)
    def _():
        for lo, hi in _chunks():
            z = _dot(xn_s[...], w_ref[:, lo:hi])
            sg_ref[:, lo:hi] = _silu(z)


def _forget_logits(xn_s, wf_ref, bf_ref):
    z = _dot(xn_s[...], wf_ref[...]) + bf_ref[...]
    return _log_sigmoid(z)


def _proj_prompt_kernel(x_ref, gain_ref, w_ref, wf_ref, bf_ref, qg_ref, kg_ref, cw_ref,
                        q_ref, k32_ref, k16_ref, v32_ref, v16_ref, logf_ref, frow_ref, sg_ref, yc_ref, cst_ref,
                        xn_s, xc_s, b_s, u_s, fcarry_s, *, tm):
    i = pl.program_id(0)
    j = pl.program_id(1)

    @pl.when(j == 0)
    def _():
        @pl.when(i == 0)
        def _():
            fcarry_s[...] = jnp.zeros_like(fcarry_s)
            u_s[0:8, :] = jnp.zeros((8, D_CONV), F32)

        xn_s[...] = _normed_x(x_ref, gain_ref)
        lf = _forget_logits(xn_s, wf_ref, bf_ref)
        logf_ref[...] = lf[:, :H_ATTN]
        lane = lax.broadcasted_iota(jnp.int32, lf.shape, 1)
        a, b, c = _split3(jnp.where(lane < H_ATTN, lf, 0.0))
        row = lax.broadcasted_iota(jnp.int32, (tm, tm), 0)
        col = lax.broadcasted_iota(jnp.int32, (tm, tm), 1)
        tri = jnp.where(row >= col, 1.0, 0.0).astype(BF16)
        fc = _dot(tri, a) + _dot(tri, b) + _dot(tri, c) + fcarry_s[...]
        fcarry_s[...] = fc[tm - 1:tm, :]
        frow_ref[...] = fc.T[:H_ATTN, :]

    _qkv_gate_epilogues(j, xn_s, w_ref, qg_ref, kg_ref, q_ref, k32_ref, k16_ref, v32_ref, v16_ref, sg_ref)

    @pl.when(j == 4)
    def _():
        for lo, hi in _chunks():
            xc_s[:, lo:hi] = _dot(xn_s[...], w_ref[:, lo:hi])

    @pl.when(j == 5)
    def _():
        for lo, hi in _chunks():
            b_s[:, lo:hi] = _dot(xn_s[...], w_ref[:, lo:hi])

    @pl.when(j == 6)
    def _():
        for lo, hi in _chunks():
            u = _dot(xn_s[...], w_ref[:, lo:hi]) * xc_s[:, lo:hi]
            u_s[8:8 + tm, lo:hi] = u
            u1 = u_s[7:7 + tm, lo:hi]
            u2 = u_s[6:6 + tm, lo:hi]
            conv = cw_ref[0:1, lo:hi] * u2 + cw_ref[1:2, lo:hi] * u1 + cw_ref[2:3, lo:hi] * u
            xc_s[:, lo:hi] = b_s[:, lo:hi] * conv
            cst_ref[:, lo:hi] = u_s[tm + 6:tm + 8, lo:hi]
            u_s[0:8, lo:hi] = u_s[tm:tm + 8, lo:hi]

    @pl.when(j == 7)
    def _():
        for lo, hi in _chunks():
            z = _dot(xn_s[...], w_ref[:, lo:hi])
            yc_ref[:, lo:hi] = (xc_s[:, lo:hi] * _silu(z)).astype(yc_ref.dtype)


def _proj_sample_kernel(x_ref, gain_ref, w_ref, wf_ref, bf_ref, qg_ref, kg_ref, cw_ref, st_ref,
                        q_ref, k32_ref, v32_ref, logf_ref, sg_ref, yc_ref, cst_ref,
                        xn_s, xc_s, b_s):
    j = pl.program_id(0)

    @pl.when(j == 0)
    def _():
        xn_s[...] = _normed_x(x_ref, gain_ref)
        logf_ref[...] = _forget_logits(xn_s, wf_ref, bf_ref)

    _qkv_gate_epilogues(j, xn_s, w_ref, qg_ref, kg_ref, q_ref, k32_ref, None, v32_ref, None, sg_ref)

    @pl.when(j == 4)
    def _():
        for lo, hi in _chunks():
            xc_s[:, lo:hi] = _dot(xn_s[...], w_ref[:, lo:hi])

    @pl.when(j == 5)
    def _():
        for lo, hi in _chunks():
            b_s[:, lo:hi] = _dot(xn_s[...], w_ref[:, lo:hi])

    @pl.when(j == 6)
    def _():
        for lo, hi in _chunks():
            u = _dot(xn_s[...], w_ref[:, lo:hi]) * xc_s[:, lo:hi]
            u2 = st_ref[:, lo:hi]
            u1 = st_ref[:, D_CONV + lo:D_CONV + hi]
            conv = cw_ref[0:1, lo:hi] * u2 + cw_ref[1:2, lo:hi] * u1 + cw_ref[2:3, lo:hi] * u
            xc_s[:, lo:hi] = b_s[:, lo:hi] * conv
            cst_ref[:, lo:hi] = u1
            cst_ref[:, D_CONV + lo:D_CONV + hi] = u

    @pl.when(j == 7)
    def _():
        for lo, hi in _chunks():
            z = _dot(xn_s[...], w_ref[:, lo:hi])
            yc_ref[:, lo:hi] = (xc_s[:, lo:hi] * _silu(z)).astype(yc_ref.dtype)


def _attn_prompt_kernel(q_ref, k_ref, v_ref, frow_ref, sg_ref, o_ref, *, tq, tk):
    qi = pl.program_id(1)
    q = q_ref[...]

    def block(jb, carry, masked):
        m, l, acc = carry
        start = pl.multiple_of(jb * tk, tk)
        ks = k_ref[pl.ds(start, tk), :]
        vs = v_ref[pl.ds(start, tk), :]
        t = _dot_nt(q, ks) - frow_ref[:, pl.ds(start, tk)]
        if masked:
            row = lax.broadcasted_iota(jnp.int32, (tq, tk), 0)
            col = lax.broadcasted_iota(jnp.int32, (tq, tk), 1)
            t = jnp.where(col <= row, t, -jnp.inf)
        m_new = jnp.maximum(m, jnp.max(t, axis=-1, keepdims=True))
        alpha = jnp.exp(m - m_new)
        p = jnp.exp(t - m_new)
        l = alpha * l + jnp.sum(p, axis=-1, keepdims=True)
        acc = alpha * acc + _dot(p.astype(BF16), vs)
        return m_new, l, acc

    init = (jnp.full((tq, 1), -jnp.inf, F32), jnp.zeros((tq, 1), F32), jnp.zeros((tq, HEAD_DIM), F32))
    carry = lax.fori_loop(0, qi, lambda jb, c: block(jb, c, False), init)
    m, l, acc = block(qi, carry, True)
    o_ref[...] = ((acc / l) * sg_ref[...]).astype(o_ref.dtype)


def _merge_kernel(x_ref, ya_ref, yc_ref, p_ref, wo_ref, wpg_ref, wple_ref, y_ref, h_s, hb_s):
    ya = ya_ref[...].astype(BF16)
    yc = yc_ref[...].astype(BF16)
    for c in range(D_MODEL // CHUNK):
        lo, hi = c * CHUNK, (c + 1) * CHUNK
        h = x_ref[:, lo:hi] + _dot(ya, wo_ref[0:D_ATTN, lo:hi]) + _dot(yc, wo_ref[D_ATTN:D_MODEL, lo:hi])
        h_s[:, lo:hi] = h
        hb_s[:, lo:hi] = h.astype(BF16)
    pb = p_ref[...].astype(BF16)
    for c in range(D_MODEL // CHUNK):
        lo, hi = c * CHUNK, (c + 1) * CHUNK
        g = jax.nn.sigmoid(_dot(hb_s[...], wpg_ref[:, lo:hi]))
        y_ref[:, lo:hi] = h_s[:, lo:hi] + g * _dot(pb, wple_ref[:, lo:hi])


def _decode_kernel(pt_ref, q_ref, kn_ref, vn_ref, lfn_ref, sg_ref, ck_hbm, cv_hbm, clf_hbm, o_ref,
                   kbuf, vbuf, lfbuf, sems, *, n_batch, n_pages, ch, nbuf):
    per_b = n_pages // ch
    n_it = n_batch * per_b
    wide = H_ATTN * HEAD_DIM

    def copies(it, slot):
        b = it // per_b
        base = b * n_pages + (it % per_b) * ch
        out = []
        for c in range(ch):
            pid = pt_ref[base + c]
            out.append(pltpu.make_async_copy(ck_hbm.at[pid], kbuf.at[slot, c], sems.at[0, slot]))
            out.append(pltpu.make_async_copy(cv_hbm.at[pid], vbuf.at[slot, c], sems.at[1, slot]))
            out.append(pltpu.make_async_copy(clf_hbm.at[pid], lfbuf.at[slot, c], sems.at[2, slot]))
        return out

    for it0 in range(nbuf - 1):
        for cp in copies(it0, it0):
            cp.start()

    sub = lax.broadcasted_iota(jnp.int32, (H_ATTN, wide), 0)
    lane_head = lax.broadcasted_iota(jnp.int32, (H_ATTN, wide), 1) // HEAD_DIM
    diag = sub == lane_head
    srow = lax.broadcasted_iota(jnp.int32, (PAGE, PAGE), 0)
    scol = lax.broadcasted_iota(jnp.int32, (PAGE, PAGE), 1)
    upper = jnp.where(srow <= scol, 1.0, 0.0).astype(BF16)

    def per_seq(b, _):
        qrow = q_ref[pl.ds(b, 1), :]
        qb = jnp.where(diag, jnp.broadcast_to(qrow, (H_ATTN, wide)), 0.0)
        qb16 = qb.astype(BF16)

        def per_chunk(cidx, carry):
            m, l, acc, fcar = carry
            it = b * per_b + cidx
            slot = lax.rem(it, nbuf)
            nxt = it + (nbuf - 1)

            @pl.when(nxt < n_it)
            def _():
                for cp in copies(nxt, lax.rem(nxt, nbuf)):
                    cp.start()

            for cp in copies(it, slot):
                cp.wait()

            ts = []
            for c in range(ch):
                s = _dot_nt(qb16, kbuf[slot, c].astype(BF16))
                a3 = _split3(lfbuf[slot, c])
                fp = fcar + _dot(a3[0], upper) + _dot(a3[1], upper) + _dot(a3[2], upper)
                fcar = fp[:, PAGE - 1:PAGE]
                ts.append(s - fp)
            t = jnp.concatenate(ts, axis=-1)
            m_new = jnp.maximum(m, jnp.max(t, axis=-1, keepdims=True))
            alpha = jnp.exp(m - m_new)
            p = jnp.exp(t - m_new)
            l = alpha * l + jnp.sum(p, axis=-1, keepdims=True)
            pv = jnp.zeros((H_ATTN, wide), F32)
            for c in range(ch):
                pv = pv + _dot(p[:, c * PAGE:(c + 1) * PAGE].astype(BF16), vbuf[slot, c].astype(BF16))
            return m_new, l, alpha * acc + pv, fcar

        init = (jnp.full((H_ATTN, 1), -jnp.inf, F32), jnp.zeros((H_ATTN, 1), F32),
                jnp.zeros((H_ATTN, wide), F32), jnp.zeros((H_ATTN, 1), F32))
        m, l, acc, fcar = lax.fori_loop(0, per_b, per_chunk, init)

        knb = jnp.broadcast_to(kn_ref[pl.ds(b, 1), :], (H_ATTN, wide))
        s_new = jnp.sum(jnp.where(diag, qb * knb, 0.0), axis=-1, keepdims=True)
        lf_row = jnp.broadcast_to(lfn_ref[pl.ds(b, 1), :], (H_ATTN, LANES))
        hsel = lax.broadcasted_iota(jnp.int32, (H_ATTN, LANES), 0) == lax.broadcasted_iota(jnp.int32, (H_ATTN, LANES), 1)
        lf_new = jnp.sum(jnp.where(hsel, lf_row, 0.0), axis=-1, keepdims=True)
        t_new = s_new - (fcar + lf_new)
        m_new = jnp.maximum(m, t_new)
        alpha = jnp.exp(m - m_new)
        p_new = jnp.exp(t_new - m_new)
        l = alpha * l + p_new
        vnb = jnp.broadcast_to(vn_ref[pl.ds(b, 1), :], (H_ATTN, wide))
        acc = alpha * acc + p_new * vnb
        out = jnp.sum(jnp.where(diag, acc / l, 0.0), axis=0, keepdims=True)
        o_ref[pl.ds(b, 1), :] = out * sg_ref[pl.ds(b, 1), :]
        return 0

    lax.fori_loop(0, n_batch, per_seq, 0)


def _params(sem):
    return pltpu.CompilerParams(dimension_semantics=sem, vmem_limit_bytes=VMEM_LIMIT)


def _const_spec(shape):
    return pl.BlockSpec(shape, lambda *_: (0,) * len(shape))


def _proj_prompt(x, gain, wmain, wf, bfp, qg, kg, cw, tm):
    s = x.shape[0]
    row_blk = lambda w: pl.BlockSpec((tm, w), lambda i, j: (i, 0))
    in_specs = [
        row_blk(D_MODEL),
        _const_spec((1, D_MODEL)),
        pl.BlockSpec((D_MODEL, D_ATTN), lambda i, j: (0, j)),
        _const_spec((D_MODEL, LANES)),
        _const_spec((1, LANES)),
        _const_spec((1, HEAD_DIM)),
        _const_spec((1, HEAD_DIM)),
        _const_spec((3, D_CONV)),
    ]
    out_shape = [
        jax.ShapeDtypeStruct((s, D_ATTN), BF16),
        jax.ShapeDtypeStruct((s, D_ATTN), F32),
        jax.ShapeDtypeStruct((s, D_ATTN), BF16),
        jax.ShapeDtypeStruct((s, D_ATTN), F32),
        jax.ShapeDtypeStruct((s, D_ATTN), BF16),
        jax.ShapeDtypeStruct((s, H_ATTN), F32),
        jax.ShapeDtypeStruct((H_ATTN, s), F32),
        jax.ShapeDtypeStruct((s, D_ATTN), F32),
        jax.ShapeDtypeStruct((s, D_CONV), BF16),
        jax.ShapeDtypeStruct((2, D_CONV), F32),
    ]
    out_specs = [
        row_blk(D_ATTN), row_blk(D_ATTN), row_blk(D_ATTN), row_blk(D_ATTN), row_blk(D_ATTN),
        row_blk(H_ATTN),
        pl.BlockSpec((H_ATTN, tm), lambda i, j: (0, i)),
        row_blk(D_ATTN), row_blk(D_CONV),
        _const_spec((2, D_CONV)),
    ]
    scratch = [
        pltpu.VMEM((tm, D_MODEL), BF16),
        pltpu.VMEM((tm, D_CONV), F32),
        pltpu.VMEM((tm, D_CONV), F32),
        pltpu.VMEM((tm + 8, D_CONV), F32),
        pltpu.VMEM((1, LANES), F32),
    ]
    return pl.pallas_call(
        functools.partial(_proj_prompt_kernel, tm=tm),
        grid=(s // tm, N_GROUPS),
        in_specs=in_specs, out_specs=out_specs, out_shape=out_shape, scratch_shapes=scratch,
        compiler_params=_params(("arbitrary", "arbitrary")),
        name="proj_prompt",
    )(x, gain, wmain, wf, bfp, qg, kg, cw)


def _proj_sample(x, gain, wmain, wf, bfp, qg, kg, cw, st):
    n = x.shape[0]
    full = lambda w: _const_spec((n, w))
    in_specs = [
        full(D_MODEL),
        _const_spec((1, D_MODEL)),
        pl.BlockSpec((D_MODEL, D_ATTN), lambda j: (0, j)),
        _const_spec((D_MODEL, LANES)),
        _const_spec((1, LANES)),
        _const_spec((1, HEAD_DIM)),
        _const_spec((1, HEAD_DIM)),
        _const_spec((3, D_CONV)),
        full(2 * D_CONV),
    ]
    out_shape = [
        jax.ShapeDtypeStruct((n, D_ATTN), F32),
        jax.ShapeDtypeStruct((n, D_ATTN), F32),
        jax.ShapeDtypeStruct((n, D_ATTN), F32),
        jax.ShapeDtypeStruct((n, LANES), F32),
        jax.ShapeDtypeStruct((n, D_ATTN), F32),
        jax.ShapeDtypeStruct((n, D_CONV), F32),
        jax.ShapeDtypeStruct((n, 2 * D_CONV), F32),
    ]
    out_specs = [full(D_ATTN), full(D_ATTN), full(D_ATTN), full(LANES), full(D_ATTN), full(D_CONV),
                 full(2 * D_CONV)]
    scratch = [
        pltpu.VMEM((n, D_MODEL), BF16),
        pltpu.VMEM((n, D_CONV), F32),
        pltpu.VMEM((n, D_CONV), F32),
    ]
    return pl.pallas_call(
        _proj_sample_kernel,
        grid=(N_GROUPS,),
        in_specs=in_specs, out_specs=out_specs, out_shape=out_shape, scratch_shapes=scratch,
        compiler_params=_params(("arbitrary",)),
        name="proj_sample",
    )(x, gain, wmain, wf, bfp, qg, kg, cw, st)


def _attn_prompt(q16, k16, v16, frow, sg, tq):
    s = q16.shape[0]
    head_tile = pl.BlockSpec((tq, HEAD_DIM), lambda h, i: (i, h))
    head_all = pl.BlockSpec((s, HEAD_DIM), lambda h, i: (0, h))
    return pl.pallas_call(
        functools.partial(_attn_prompt_kernel, tq=tq, tk=tq),
        grid=(H_ATTN, s // tq),
        in_specs=[head_tile, head_all, head_all,
                  pl.BlockSpec((None, 1, s), lambda h, i: (h, 0, 0)),
                  head_tile],
        out_specs=head_tile,
        out_shape=jax.ShapeDtypeStruct((s, D_ATTN), BF16),
        compiler_params=_params(("arbitrary", "arbitrary")),
        name="attn_prompt",
    )(q16, k16, v16, frow.reshape(H_ATTN, 1, s), sg)


def _merge(x, ya, yc, p, wo, wpg, wple, tm):
    m = x.shape[0]
    row_blk = lambda w: pl.BlockSpec((tm, w), lambda i: (i, 0))
    resident = lambda shape: pl.BlockSpec(shape, lambda i: (0, 0), pipeline_mode=pl.Buffered(1))
    return pl.pallas_call(
        _merge_kernel,
        grid=(m // tm,),
        in_specs=[row_blk(D_MODEL), row_blk(D_ATTN), row_blk(D_CONV), row_blk(PLE_DIM),
                  resident((D_MODEL, D_MODEL)), resident((D_MODEL, D_MODEL)), resident((PLE_DIM, D_MODEL))],
        out_specs=row_blk(D_MODEL),
        out_shape=jax.ShapeDtypeStruct((m, D_MODEL), F32),
        scratch_shapes=[pltpu.VMEM((tm, D_MODEL), F32), pltpu.VMEM((tm, D_MODEL), BF16)],
        compiler_params=_params(("arbitrary",)),
        name="merge",
    )(x, ya, yc, p, wo, wpg, wple)


def _decode(page_table, q, kn, vn, lfn, sg, ck, cv, clf_t, ch, nbuf):
    n_batch, n_pages = page_table.shape
    wide = H_ATTN * HEAD_DIM
    vmem = pl.BlockSpec(memory_space=pltpu.VMEM)
    hbm = pl.BlockSpec(memory_space=pl.ANY)
    grid_spec = pltpu.PrefetchScalarGridSpec(
        num_scalar_prefetch=1,
        grid=(1,),
        in_specs=[vmem, vmem, vmem, vmem, vmem, hbm, hbm, hbm],
        out_specs=vmem,
        scratch_shapes=[
            pltpu.VMEM((nbuf, ch, PAGE, wide), F32),
            pltpu.VMEM((nbuf, ch, PAGE, wide), F32),
            pltpu.VMEM((nbuf, ch, H_ATTN, PAGE), F32),
            pltpu.SemaphoreType.DMA((3, nbuf)),
        ],
    )
    return pl.pallas_call(
        functools.partial(_decode_kernel, n_batch=n_batch, n_pages=n_pages, ch=ch, nbuf=nbuf),
        grid_spec=grid_spec,
        out_shape=jax.ShapeDtypeStruct((n_batch, wide), F32),
        compiler_params=_params(("arbitrary",)),
        name="decode_attn",
    )(page_table.reshape(-1), q, kn, vn, lfn, sg, ck, cv, clf_t)


def kernel(x_prompt, x_sample, cache_k, cache_v, cache_logf, state_conv, page_table, p_prompt, p_sample,
           norm_gain, w_in, b_forget, q_norm_gain, k_norm_gain, conv_w, w_out, w_ple, w_ple_gate):
    assert w_in.shape[0] == 1 and x_prompt.shape[0] == 1 and x_sample.shape[1] == 1
    seq = x_prompt.shape[1]
    n_dec = x_sample.shape[0]
    n_pool = cache_k.shape[1]

    w = w_in[0]
    f_lo, f_hi = 3 * D_ATTN, 3 * D_ATTN + H_ATTN
    wmain = jnp.concatenate([w[:, :f_lo], w[:, f_hi:]], axis=1).astype(BF16)
    wf = jnp.pad(w[:, f_lo:f_hi], ((0, 0), (0, LANES - H_ATTN))).astype(BF16)
    bfp = jnp.pad(b_forget[0], (0, LANES - H_ATTN)).reshape(1, LANES)
    gain = norm_gain[0].reshape(1, D_MODEL)
    qg = q_norm_gain[0].reshape(1, HEAD_DIM)
    kg = k_norm_gain[0].reshape(1, HEAD_DIM)
    cw = conv_w[0]
    wo = w_out[0].astype(BF16)
    wpg = w_ple_gate[0].astype(BF16)
    wple = w_ple[0].astype(BF16)

    xp = x_prompt.reshape(seq, D_MODEL)
    q16, k32, k16, v32, v16, logf_p, frow, sg_p, yc_p, cst_p = _proj_prompt(
        xp, gain, wmain, wf, bfp, qg, kg, cw, tm=512)
    ya_p = _attn_prompt(q16, k16, v16, frow, sg_p, tq=512)
    y_p = _merge(xp, ya_p, yc_p, p_prompt.reshape(seq, PLE_DIM), wo, wpg, wple, tm=512)

    xs = x_sample.reshape(n_dec, D_MODEL)
    st = state_conv.reshape(n_dec, 2 * D_CONV)
    q_s, k_s, v_s, lf_s, sg_s, yc_s, cst_s = _proj_sample(xs, gain, wmain, wf, bfp, qg, kg, cw, st)
    ck = cache_k.reshape(n_pool, PAGE, D_ATTN)
    cv = cache_v.reshape(n_pool, PAGE, D_ATTN)
    clf_t = jnp.swapaxes(cache_logf[0], 1, 2)
    ya_s = _decode(page_table, q_s, k_s, v_s, lf_s, sg_s, ck, cv, clf_t, ch=4, nbuf=3)
    y_s = _merge(xs, ya_s, yc_s, p_sample.reshape(n_dec, PLE_DIM), wo, wpg, wple, tm=n_dec)

    return (
        y_p.reshape(1, seq, D_MODEL),
        y_s.reshape(n_dec, 1, D_MODEL),
        k32.reshape(1, 1, seq, H_ATTN, HEAD_DIM),
        v32.reshape(1, 1, seq, H_ATTN, HEAD_DIM),
        logf_p.reshape(1, 1, seq, H_ATTN),
        cst_p.reshape(1, 1, 2, D_CONV),
        k_s.reshape(1, n_dec, 1, H_ATTN, HEAD_DIM),
        v_s.reshape(1, n_dec, 1, H_ATTN, HEAD_DIM),
        lf_s[:, :H_ATTN].reshape(1, n_dec, 1, H_ATTN),
        cst_s.reshape(1, n_dec, 2, D_CONV),
    )
```
